```python
import math
import jax, jax.numpy as jnp
from jax import lax
import numpy as np

D_MODEL = 1024
BATCH = 16
SEQ = 4096
DEPTH = 2

PLE_DIM = 256
EPS = 1e-6
CONV_WIDTH = 4
LRU_WIDTH = 1024
LRU_BLOCKS = 8
LRU_BLOCK = LRU_WIDTH // LRU_BLOCKS
LRU_C = 8.0
ML_HEADS = 4
ML_HEAD_DIM = 256
ML_WIDTH = ML_HEADS * ML_HEAD_DIM
ML_CHUNK = 128
SB_HEADS = 16
SB_HEAD_DIM = 64
SB_WIDTH = SB_HEADS * SB_HEAD_DIM
SB_BLOCK = 128
PEER_HEADS = 8
PEER_KEY_DIM = 256
PEER_HALF = PEER_KEY_DIM // 2
N_KEYS = 128
N_EXPERTS = N_KEYS * N_KEYS
PEER_TOPK = 16
PEER_TOKEN_BLOCK = 512

N_EVEN = (DEPTH + 1) // 2
N_ODD = DEPTH // 2
SPLITS = (LRU_WIDTH, 2 * LRU_WIDTH, 2 * LRU_WIDTH + ML_WIDTH, 2 * LRU_WIDTH + 2 * ML_WIDTH,
          2 * LRU_WIDTH + 3 * ML_WIDTH, 2 * LRU_WIDTH + 3 * ML_WIDTH + ML_HEADS)
IN_COLS = 2 * LRU_WIDTH + 3 * ML_WIDTH + 2 * ML_HEADS
MIX_EVEN = LRU_WIDTH + ML_WIDTH

kernel_name = 'hybrid_rglru_mlstm_stickbreak_peer'

f32 = jnp.float32


def rmsnorm(x, g):
    xf = x.astype(f32)
    y = xf * lax.rsqrt(jnp.mean(xf * xf, axis=-1, keepdims=True) + EPS)
    return (y * g.astype(f32)).astype(x.dtype)


def causal_depthwise_conv(x, w, b):
    C = x.shape[-1]
    y = lax.conv_general_dilated(x, w[:, None, :].astype(x.dtype), window_strides=(1,),
                                 padding=[(CONV_WIDTH - 1, 0)],
                                 dimension_numbers=('NWC', 'WIO', 'NWC'),
                                 feature_group_count=C)
    return y + b


def block_diag_linear(x, w, b):
    nb, bs, _ = w.shape
    xs = x.reshape(x.shape[:-1] + (nb, bs))
    y = jnp.einsum('bsni,nij->bsnj', xs, w)
    return y.reshape(x.shape) + b


def rg_lru(x, w_r, b_r, w_i, b_i, lam):
    S = x.shape[1]
    r = jax.nn.sigmoid(block_diag_linear(x, w_r, b_r).astype(f32))
    i = jax.nn.sigmoid(block_diag_linear(x, w_i, b_i).astype(f32))
    log_a = -LRU_C * r * jax.nn.softplus(-lam.astype(f32))
    a = jnp.exp(log_a)
    mult = jnp.sqrt(-jnp.expm1(2.0 * log_a))
    mult = jnp.where((jnp.arange(S) == 0)[None, :, None], 1.0, mult)
    bterm = mult * i * x.astype(f32)

    def combine(left, right):
        a_l, b_l = left
        a_r, b_r2 = right
        return a_l * a_r, a_r * b_l + b_r2

    _, h = lax.associative_scan(combine, (a, bterm), axis=1)
    return h.astype(x.dtype)


def mlstm_chunkwise(q, k, v, i_pre, logf):
    Bn, H, S, d = q.shape
    L = ML_CHUNK
    nc = S // L

    def to_chunks(t):
        t = t.reshape(t.shape[:2] + (nc, L) + t.shape[3:])
        return jnp.moveaxis(t, 2, 0)

    qc, kc, vc = to_chunks(q.astype(f32)), to_chunks(k.astype(f32)), to_chunks(v.astype(f32))
    ic, fc = to_chunks(i_pre), to_chunks(logf)
    causal = jnp.tril(jnp.ones((L, L), dtype=bool))

    def step(carry, inp):
        C, n, m = carry
        q_, k_, v_, i_, f_ = inp
        b = jnp.cumsum(f_, axis=-1)
        g = b[..., -1]
        dlog = jnp.where(causal, b[..., :, None] - b[..., None, :] + i_[..., None, :], -jnp.inf)
        inter = b + m[..., None]
        m_t = jnp.maximum(inter, jnp.max(dlog, axis=-1))
        sc = jnp.einsum('bhtd,bhsd->bhts', q_, k_) * jnp.exp(dlog - m_t[..., None])
        w_inter = jnp.exp(inter - m_t)
        num = jnp.einsum('bhts,bhsv->bhtv', sc, v_) + w_inter[..., None] * jnp.einsum('bhvk,bhtk->bhtv', C, q_)
        den = jnp.sum(sc, axis=-1) + w_inter * jnp.einsum('bhk,bhtk->bht', n, q_)
        h = num / jnp.maximum(jnp.abs(den), jnp.exp(-m_t))[..., None]
        decay_s = g[..., None] - b + i_
        m_new = jnp.maximum(g + m, jnp.max(decay_s, axis=-1))
        ws = jnp.exp(decay_s - m_new[..., None])
        wc = jnp.exp(g + m - m_new)
        C_new = wc[..., None, None] * C + jnp.einsum('bhs,bhsv,bhsk->bhvk', ws, v_, k_)
        n_new = wc[..., None] * n + jnp.einsum('bhs,bhsk->bhk', ws, k_)
        return (C_new, n_new, m_new), h

    init = (jnp.zeros((Bn, H, d, d), f32), jnp.zeros((Bn, H, d), f32), jnp.zeros((Bn, H), f32))
    _, hs = lax.scan(step, init, (qc, kc, vc, ic, fc))
    return jnp.moveaxis(hs, 0, 2).reshape(Bn, H, S, d)


def stick_breaking_attention(q, k, v):
    Bn, H, S, d = q.shape
    nb = S // SB_BLOCK
    scale = d ** -0.5

    def blocks(t):
        return jnp.moveaxis(t.reshape(Bn, H, nb, SB_BLOCK, d), 2, 0)

    qb, kb, vb = blocks(q), blocks(k), blocks(v)
    pos = jnp.arange(SB_BLOCK)

    def one_query_block(args):
        qi, q_blk = args
        t_pos = qi * SB_BLOCK + pos

        def body(step, carry):
            acc, out = carry
            kj = qi - step
            k_blk = lax.dynamic_index_in_dim(kb, kj, 0, keepdims=False)
            v_blk = lax.dynamic_index_in_dim(vb, kj, 0, keepdims=False)
            s_pos = kj * SB_BLOCK + pos
            valid = s_pos[None, :] < t_pos[:, None]
            z = (jnp.einsum('bhqd,bhkd->bhqk', q_blk, k_blk) * scale).astype(f32)
            log_beta = jax.nn.log_sigmoid(z)
            log_1mb = jnp.where(valid, jax.nn.log_sigmoid(-z), 0.0)
            row = jnp.sum(log_1mb, axis=-1)
            suffix = row[..., None] - jnp.cumsum(log_1mb, axis=-1)
            A = jnp.where(valid, jnp.exp(log_beta + suffix + acc[..., None]), 0.0)
            out = out + jnp.einsum('bhqk,bhkd->bhqd', A, v_blk.astype(f32))
            return acc + row, out

        init = (jnp.zeros((Bn, H, SB_BLOCK), f32), jnp.zeros((Bn, H, SB_BLOCK, d), f32))
        _, out = lax.fori_loop(0, qi + 1, body, init)
        return out

    outs = lax.map(one_query_block, (jnp.arange(nb), qb))
    return jnp.moveaxis(outs, 0, 2).reshape(Bn, H, S, d).astype(q.dtype)


def even_mixer(xn, w_in, conv_a_w, conv_a_b, lru_wr, lru_br, lru_wi, lru_bi, lru_lam,
               conv_m_w, conv_m_b, ml_wq, ml_wk, ml_bi, ml_bf, ml_hnorm_g, w_out):
    Bn, S, _ = xn.shape
    proj = xn @ w_in
    xa, ya, xm, vm, om, ig, fg = jnp.split(proj, SPLITS, axis=-1)
    ha = rg_lru(causal_depthwise_conv(xa, conv_a_w, conv_a_b), lru_wr, lru_br, lru_wi, lru_bi, lru_lam)
    out_a = ha * jax.nn.gelu(ya)
    xc = jax.nn.silu(causal_depthwise_conv(xm, conv_m_w, conv_m_b)).reshape(Bn, S, ML_HEADS, ML_HEAD_DIM)
    q = jnp.einsum('bshi,hij->bhsj', xc, ml_wq)
    k = jnp.einsum('bshi,hij->bhsj', xc, ml_wk) * (ML_HEAD_DIM ** -0.5)
    v = vm.reshape(Bn, S, ML_HEADS, ML_HEAD_DIM).transpose(0, 2, 1, 3)
    i_pre = (ig + ml_bi).astype(f32).transpose(0, 2, 1)
    logf = jax.nn.log_sigmoid((fg + ml_bf).astype(f32)).transpose(0, 2, 1)
    hb = mlstm_chunkwise(q, k, v, i_pre, logf).transpose(0, 2, 1, 3)
    hb = rmsnorm(hb.astype(xn.dtype), ml_hnorm_g)
    out_b = jax.nn.sigmoid(om) * hb.reshape(Bn, S, ML_WIDTH)
    return jnp.concatenate([out_a, out_b], axis=-1) @ w_out


def odd_mixer(xn, w_qkv, q_g, k_g, w_out):
    Bn, S, _ = xn.shape
    qkv = (xn @ w_qkv).reshape(Bn, S, 3, SB_HEADS, SB_HEAD_DIM)
    q = rmsnorm(qkv[:, :, 0], q_g).transpose(0, 2, 1, 3)
    k = rmsnorm(qkv[:, :, 1], k_g).transpose(0, 2, 1, 3)
    v = qkv[:, :, 2].transpose(0, 2, 1, 3)
    o = stick_breaking_attention(q, k, v)
    return o.transpose(0, 2, 1, 3).reshape(Bn, S, SB_WIDTH) @ w_out


def peer(xn, w_q, sub_keys, u, v):
    Bn, S, D = xn.shape
    T = Bn * S
    blk = math.gcd(T, PEER_TOKEN_BLOCK)
    xt = xn.reshape(T // blk, blk, D)

    def one_block(xb):
        q = (xb @ w_q).reshape(blk, PEER_HEADS, 2, PEER_HALF)
        s = jnp.einsum('thpc,pnc->thpn', q, sub_keys).astype(f32)
        s1, i1 = lax.top_k(s[:, :, 0], PEER_TOPK)
        s2, i2 = lax.top_k(s[:, :, 1], PEER_TOPK)
        cand = (s1[..., :, None] + s2[..., None, :]).reshape(blk, PEER_HEADS, PEER_TOPK * PEER_TOPK)
        cand_idx = (i1[..., :, None] * N_KEYS + i2[..., None, :]).reshape(blk, PEER_HEADS, PEER_TOPK * PEER_TOPK)
        top_s, top_pos = lax.top_k(cand, PEER_TOPK)
        idx = jnp.take_along_axis(cand_idx, top_pos, axis=-1)
        g = jax.nn.softmax(top_s, axis=-1)
        act = jax.nn.gelu(jnp.einsum('td,thkd->thk', xb, u[idx]).astype(f32))
        coef = (g * act).astype(xb.dtype)
        return jnp.einsum('thk,thkd->td', coef, v[idx])

    return lax.map(one_block, xt).reshape(Bn, S, D)


def setup_inputs(seed: int = 0) -> dict:
    key = jax.random.key(seed)
    ks = iter(jax.random.split(key, 64))

    def nrm(shape, scale):
        return jax.random.normal(next(ks), shape, f32) * scale

    def gain(shape):
        return 1.0 + 0.01 * jax.random.normal(next(ks), shape, f32)

    D = D_MODEL
    lam_u = jax.random.uniform(next(ks), (N_EVEN, LRU_WIDTH), f32, minval=0.9, maxval=0.999)
    a0 = lam_u ** (1.0 / LRU_C)
    return {
        'x': nrm((BATCH, SEQ, D), 1.0),
        'p': nrm((DEPTH, BATCH, SEQ, PLE_DIM), 1.0),
        'ln_mix_g': gain((DEPTH, D)),
        'ln_ffn_g': gain((DEPTH, D)),
        'w_in': nrm((N_EVEN, D, IN_COLS), D ** -0.5),
        'conv_a_w': nrm((N_EVEN, CONV_WIDTH, LRU_WIDTH), CONV_WIDTH ** -0.5),
        'conv_a_b': nrm((N_EVEN, LRU_WIDTH), 0.01),
        'lru_wr': nrm((N_EVEN, LRU_BLOCKS, LRU_BLOCK, LRU_BLOCK), LRU_BLOCK ** -0.5),
        'lru_br': nrm((N_EVEN, LRU_WIDTH), 0.01),
        'lru_wi': nrm((N_EVEN, LRU_BLOCKS, LRU_BLOCK, LRU_BLOCK), LRU_BLOCK ** -0.5),
        'lru_bi': nrm((N_EVEN, LRU_WIDTH), 0.01),
        'lru_lam': jnp.log(a0) - jnp.log1p(-a0),
        'conv_m_w': nrm((N_EVEN, CONV_WIDTH, ML_WIDTH), CONV_WIDTH ** -0.5),
        'conv_m_b': nrm((N_EVEN, ML_WIDTH), 0.01),
        'ml_wq': nrm((N_EVEN, ML_HEADS, ML_HEAD_DIM, ML_HEAD_DIM), ML_HEAD_DIM ** -0.5),
        'ml_wk': nrm((N_EVEN, ML_HEADS, ML_HEAD_DIM, ML_HEAD_DIM), ML_HEAD_DIM ** -0.5),
        'ml_bi': nrm((N_EVEN, ML_HEADS), 0.1),
        'ml_bf': jnp.linspace(3.0, 6.0, ML_HEADS, dtype=f32)[None, :] + nrm((N_EVEN, ML_HEADS), 0.1),
        'ml_hnorm_g': gain((N_EVEN, ML_HEADS, ML_HEAD_DIM)),
        'w_out_even': nrm((N_EVEN, MIX_EVEN, D), MIX_EVEN ** -0.5),
        'w_qkv': nrm((N_ODD, D, 3 * SB_WIDTH), D ** -0.5),
        'sb_q_g': gain((N_ODD, SB_HEAD_DIM)),
        'sb_k_g': gain((N_ODD, SB_HEAD_DIM)),
        'w_out_odd': nrm((N_ODD, SB_WIDTH, D), SB_WIDTH ** -0.5),
        'peer_wq': nrm((DEPTH, D, PEER_HEADS * PEER_KEY_DIM), D ** -0.5),
        'peer_subkeys': nrm((DEPTH, 2, N_KEYS, PEER_HALF), PEER_HALF ** -0.5),
        'peer_u': nrm((DEPTH, N_EXPERTS, D), D ** -0.5),
        'peer_v': nrm((DEPTH, N_EXPERTS, D), PEER_HEADS ** -0.5),
        'ple_wp': nrm((DEPTH, PLE_DIM, D), PLE_DIM ** -0.5),
        'ple_g': gain((DEPTH, D)),
        'ple_wg': nrm((DEPTH, D, D), D ** -0.5),
    }


def reference(x, p, ln_mix_g, ln_ffn_g, w_in, conv_a_w, conv_a_b, lru_wr, lru_br, lru_wi, lru_bi,
              lru_lam, conv_m_w, conv_m_b, ml_wq, ml_wk, ml_bi, ml_bf, ml_hnorm_g, w_out_even,
              w_qkv, sb_q_g, sb_k_g, w_out_odd, peer_wq, peer_subkeys, peer_u, peer_v,
              ple_wp, ple_g, ple_wg):
    h = x
    for layer in range(DEPTH):
        xn = rmsnorm(h, ln_mix_g[layer])
        if layer % 2 == 0:
            e = layer // 2
            mix = even_mixer(xn, w_in[e], conv_a_w[e], conv_a_b[e], lru_wr[e], lru_br[e], lru_wi[e],
                             lru_bi[e], lru_lam[e], conv_m_w[e], conv_m_b[e], ml_wq[e], ml_wk[e],
                             ml_bi[e], ml_bf[e], ml_hnorm_g[e], w_out_even[e])
        else:
            o = layer // 2
            mix = odd_mixer(xn, w_qkv[o], sb_q_g[o], sb_k_g[o], w_out_odd[o])
        h = h + mix
        h = h + peer(rmsnorm(h, ln_ffn_g[layer]), peer_wq[layer], peer_subkeys[layer],
                     peer_u[layer], peer_v[layer])
        ple = rmsnorm(p[layer] @ ple_wp[layer], ple_g[layer])
        h = h + jax.nn.sigmoid(h @ ple_wg[layer]) * ple
    return h
```

```python
import functools
import math

import jax
import jax.numpy as jnp
from jax import lax
from jax.experimental import pallas as pl
from jax.experimental.pallas import tpu as pltpu

F32 = jnp.float32
BF16 = jnp.bfloat16
EPS = 1e-6
NEG_INF = float("-inf")

LANES = 128
CONV_WIDTH = 4
LRU_BLOCKS = 8
LRU_C = 8.0
ML_HEADS = 4
ML_CHUNK = 128
SB_HEADS = 16
SB_HEAD_DIM = 64
SB_BLOCK = 128
PEER_HEADS = 8
N_KEYS = 128
PEER_TOPK = 16
VMEM_LIMIT = 56 * 1024 * 1024


def _cparams(sem):
    return pltpu.CompilerParams(dimension_semantics=sem, vmem_limit_bytes=VMEM_LIMIT)


def _rms(x, g):
    ms = jnp.mean(x * x, axis=-1, keepdims=True)
    return x * lax.rsqrt(ms + EPS) * g


def _gelu(x):
    c = math.sqrt(2.0 / math.pi)
    return 0.5 * x * (1.0 + jnp.tanh(c * (x + 0.044715 * (x * x * x))))


def _sigmoid(x):
    return 1.0 / (1.0 + jnp.exp(-x))


def _log_sigmoid(x):
    return jnp.minimum(x, 0.0) - jnp.log(1.0 + jnp.exp(-jnp.abs(x)))


def _dot(a, b):
    return jnp.dot(a, b, preferred_element_type=F32)


def _dot_nt(a, b):
    return lax.dot_general(a, b, (((1,), (1,)), ((), ())), preferred_element_type=F32)


def _dot_tn(a, b):
    return lax.dot_general(a, b, (((0,), (0,)), ((), ())), preferred_element_type=F32)


def _norm_mm_kernel(x_ref, g_ref, w_ref, o_ref, xn_ref):
    @pl.when(pl.program_id(1) == 0)
    def _():
        xn_ref[...] = _rms(x_ref[...], g_ref[...]).astype(BF16)

    o_ref[...] = _dot(xn_ref[...], w_ref[...]).astype(o_ref.dtype)


def norm_matmul(x, g, w, *, tm, tn, out_dtype=F32):
    T, D = x.shape
    N = w.shape[1]
    return pl.pallas_call(
        _norm_mm_kernel,
        out_shape=jax.ShapeDtypeStruct((T, N), out_dtype),
        grid=(T // tm, N // tn),
        in_specs=[pl.BlockSpec((tm, D), lambda i, j: (i, 0)),
                  pl.BlockSpec((1, D), lambda i, j: (0, 0)),
                  pl.BlockSpec((D, tn), lambda i, j: (0, j))],
        out_specs=pl.BlockSpec((tm, tn), lambda i, j: (i, j)),
        scratch_shapes=[pltpu.VMEM((tm, D), BF16)],
        compiler_params=_cparams(("parallel", "arbitrary")),
        name="norm_matmul",
    )(x, g.reshape(1, D), w)


def _qkv_kernel(x_ref, g_ref, w_ref, hg_ref, gm_ref, o_ref, xn_ref):
    j = pl.program_id(1)

    @pl.when(j == 0)
    def _():
        xn_ref[...] = _rms(x_ref[...], g_ref[...]).astype(BF16)

    y = _dot(xn_ref[...], w_ref[...])

    @pl.when(j < 2)
    def _():
        gm = gm_ref[...]
        hg = hg_ref[0]
        for c in range(y.shape[1] // LANES):
            yc = y[:, c * LANES:(c + 1) * LANES]
            ms = _dot((yc * yc).astype(BF16), gm)
            o_ref[:, c * LANES:(c + 1) * LANES] = (
                yc * lax.rsqrt(ms + EPS) * hg[:, c * LANES:(c + 1) * LANES]).astype(o_ref.dtype)

    @pl.when(j == 2)
    def _():
        o_ref[...] = y.astype(o_ref.dtype)


def qkv_proj(x, g, w, q_g, k_g, *, tm):
    T, D = x.shape
    W = SB_HEADS * SB_HEAD_DIM
    scale = SB_HEAD_DIM ** -0.5
    hg = jnp.stack([jnp.tile(q_g, SB_HEADS) * scale, jnp.tile(k_g, SB_HEADS),
                    jnp.ones((W,), F32)]).reshape(3, 1, W)
    lane = jnp.arange(LANES)
    gm = ((lane[:, None] // SB_HEAD_DIM) == (lane[None, :] // SB_HEAD_DIM)).astype(BF16) / SB_HEAD_DIM
    return pl.pallas_call(
        _qkv_kernel,
        out_shape=jax.ShapeDtypeStruct((T, 3 * W), BF16),
        grid=(T // tm, 3),
        in_specs=[pl.BlockSpec((tm, D), lambda i, j: (i, 0)),
                  pl.BlockSpec((1, D), lambda i, j: (0, 0)),
                  pl.BlockSpec((D, W), lambda i, j: (0, j)),
                  pl.BlockSpec((1, 1, W), lambda i, j: (j, 0, 0)),
                  pl.BlockSpec((LANES, LANES), lambda i, j: (0, 0))],
        out_specs=pl.BlockSpec((tm, W), lambda i, j: (i, j)),
        scratch_shapes=[pltpu.VMEM((tm, D), BF16)],
        compiler_params=_cparams(("parallel", "arbitrary")),
        name="qkv_proj",
    )(x, g.reshape(1, D), w, hg, gm.astype(BF16))


def _mm_res_kernel(*refs, n_a):
    a_refs = refs[:n_a]
    w_refs = refs[n_a:2 * n_a]
    r_ref, o_ref = refs[2 * n_a], refs[2 * n_a + 1]
    acc = r_ref[...]
    for a_ref, w_ref in zip(a_refs, w_refs):
        acc = acc + _dot(a_ref[...].astype(BF16), w_ref[...])
    o_ref[...] = acc


def matmul_residual(a_list, w_list, res, *, tm):
    T, N = res.shape
    n_a = len(a_list)
    in_specs = ([pl.BlockSpec((tm, a.shape[1]), lambda i: (i, 0)) for a in a_list]
                + [pl.BlockSpec(w.shape, lambda i: (0, 0)) for w in w_list]
                + [pl.BlockSpec((tm, N), lambda i: (i, 0))])
    return pl.pallas_call(
        functools.partial(_mm_res_kernel, n_a=n_a),
        out_shape=jax.ShapeDtypeStruct((T, N), F32),
        grid=(T // tm,),
        in_specs=in_specs,
        out_specs=pl.BlockSpec((tm, N), lambda i: (i, 0)),
        compiler_params=_cparams(("parallel",)),
        name="matmul_residual",
    )(*a_list, *w_list, res)


def _conv4(x, tail, w, b):
    ts = x.shape[0]
    xe = jnp.concatenate([tail, x], axis=0)
    y = x * w[3:4, :] + b
    for k in range(1, CONV_WIDTH):
        y = y + pltpu.roll(xe, k, 0)[8:8 + ts, :] * w[3 - k:4 - k, :]
    return y


def _rglru_kernel(xa_ref, ya_ref, cw_ref, cb_ref, wr_ref, br_ref, wi_ref, bi_ref, lam_ref,
                  o_ref, h_ref, tail_ref, a_s, b_s):
    c = pl.program_id(1)
    ts, C = xa_ref.shape
    bs = C // LRU_BLOCKS

    @pl.when(c == 0)
    def _():
        h_ref[...] = jnp.zeros_like(h_ref)
        tail_ref[...] = jnp.zeros_like(tail_ref)

    xa = xa_ref[...]
    xc = _conv4(xa, tail_ref[...], cw_ref[...], cb_ref[...])
    tail_ref[...] = xa[ts - 8:, :]

    sp = jnp.log(1.0 + jnp.exp(-jnp.abs(lam_ref[...]))) + jnp.maximum(-lam_ref[...], 0.0)
    first = jnp.logical_and(c == 0, lax.broadcasted_iota(jnp.int32, (ts, 1), 0) == 0)
    for n in range(LRU_BLOCKS):
        sl = slice(n * bs, (n + 1) * bs)
        xb = xc[:, sl]
        xb16 = xb.astype(BF16)
        r = _sigmoid(_dot(xb16, wr_ref[n]) + br_ref[:, sl])
        ig = _sigmoid(_dot(xb16, wi_ref[n]) + bi_ref[:, sl])
        log_a = (-LRU_C) * r * sp[:, sl]
        a = jnp.exp(log_a)
        mult = jnp.sqrt(1.0 - jnp.exp(2.0 * log_a))
        mult = jnp.where(first, 1.0, mult)
        a_s[:, sl] = a
        b_s[:, sl] = mult * ig * xb

    def rows8(i, h):
        r0 = pl.multiple_of(i * 8, 8)
        a8 = a_s[pl.ds(r0, 8), :]
        b8 = b_s[pl.ds(r0, 8), :]
        hs = []
        for j in range(8):
            h = a8[j:j + 1, :] * h + b8[j:j + 1, :]
            hs.append(h)
        b_s[pl.ds(r0, 8), :] = jnp.concatenate(hs, axis=0)
        return h

    h_ref[...] = lax.fori_loop(0, ts // 8, rows8, h_ref[...])
    o_ref[...] = (b_s[...] * _gelu(ya_ref[...])).astype(o_ref.dtype)


def rglru_mixer(proj, conv_w, conv_b, wr, br, wi, bi, lam, *, B, S, ts):
    C = conv_w.shape[1]
    nc = S // ts
    row = lambda v: v.reshape(1, C)
    full = lambda shp: pl.BlockSpec(shp, lambda b, c: (0,) * len(shp))
    return pl.pallas_call(
        _rglru_kernel,
        out_shape=jax.ShapeDtypeStruct((B * S, C), BF16),
        grid=(B, nc),
        in_specs=[pl.BlockSpec((ts, C), lambda b, c: (b * nc + c, 0)),
                  pl.BlockSpec((ts, C), lambda b, c: (b * nc + c, 1)),
                  full((CONV_WIDTH, C)), full((1, C)),
                  full(wr.shape), full((1, C)), full(wi.shape), full((1, C)), full((1, C))],
        out_specs=pl.BlockSpec((ts, C), lambda b, c: (b * nc + c, 0)),
        scratch_shapes=[pltpu.VMEM((1, C), F32), pltpu.VMEM((8, C), F32),
                        pltpu.VMEM((ts, C), F32), pltpu.VMEM((ts, C), F32)],
        compiler_params=_cparams(("parallel", "arbitrary")),
        name="rglru_mixer",
    )(proj, proj, conv_w, row(conv_b), wr.astype(BF16), row(br), wi.astype(BF16), row(bi), row(lam))


def _mlstm_kernel(xm_ref, vm_ref, om_ref, gt_ref, cw_ref, cb_ref, wq_ref, wk_ref, gb_ref, hg_ref,
                  tri_ref, o_ref, ct_ref, n_ref, m_ref, tail_ref):
    c = pl.program_id(1)
    L, C = xm_ref.shape
    dh = C // ML_HEADS

    @pl.when(c == 0)
    def _():
        ct_ref[...] = jnp.zeros_like(ct_ref)
        n_ref[...] = jnp.zeros_like(n_ref)
        m_ref[...] = jnp.zeros_like(m_ref)
        tail_ref[...] = jnp.zeros_like(tail_ref)

    xm = xm_ref[...]
    pre = _conv4(xm, tail_ref[...], cw_ref[...], cb_ref[...])
    tail_ref[...] = xm[L - 8:, :]
    xc = pre * _sigmoid(pre)

    g = gt_ref[...] + gb_ref[...]
    lf = _log_sigmoid(g)
    bcum = jnp.dot(tri_ref[...], lf, preferred_element_type=F32,
                   precision=lax.Precision.HIGHEST)
    g_t = g.T
    b_t = bcum.T
    ri = lax.broadcasted_iota(jnp.int32, (L, L), 0)
    ci = lax.broadcasted_iota(jnp.int32, (L, L), 1)
    causal = ci <= ri

    for h in range(ML_HEADS):
        sl = slice(h * dh, (h + 1) * dh)
        fh = ML_HEADS + h
        b_col = bcum[:, fh:fh + 1]
        i_col = g[:, h:h + 1]
        b_row = b_t[fh:fh + 1, :]
        i_row = g_t[h:h + 1, :]
        g_last = bcum[L - 1:L, fh:fh + 1]
        m_prev = m_ref[h]

        x16 = xc[:, sl].astype(BF16)
        q = _dot(x16, wq_ref[h])
        k = _dot(x16, wk_ref[h]) * (dh ** -0.5)
        v = vm_ref[:, sl]
        q16 = q.astype(BF16)
        k16 = k.astype(BF16)

        dlog = jnp.where(causal, b_col - b_row + i_row, NEG_INF)
        inter = b_col + m_prev
        m_t = jnp.maximum(inter, jnp.max(dlog, axis=-1, keepdims=True))
        sc = _dot_nt(q16, k16) * jnp.exp(dlog - m_t)
        w_inter = jnp.exp(inter - m_t)
        ct = ct_ref[h]
        n_row = n_ref[h]
        num = _dot(sc.astype(BF16), v.astype(BF16)) + w_inter * _dot(q16, ct.astype(BF16))
        den = jnp.sum(sc, axis=-1, keepdims=True) + w_inter * jnp.sum(q * n_row, axis=-1, keepdims=True)
        hh = num / jnp.maximum(jnp.abs(den), jnp.exp(-m_t))

        decay = g_last - b_col + i_col
        m_new = jnp.maximum(g_last + m_prev, jnp.max(decay, axis=0, keepdims=True))
        ws = jnp.exp(decay - m_new)
        wc = jnp.exp(g_last + m_prev - m_new)
        ct_ref[h] = wc * ct + _dot_tn(k16, (ws * v).astype(BF16))
        n_ref[h] = wc * n_row + jnp.sum(ws * k, axis=0, keepdims=True)
        m_ref[h] = m_new

        hn = _rms(hh, hg_ref[:, sl])
        o_ref[:, sl] = (_sigmoid(om_ref[:, sl]) * hn).astype(o_ref.dtype)


def mlstm_mixer(proj, gates, conv_w, conv_b, wq, wk, b_i, b_f, hnorm_g, *, B, S, col0):
    C = conv_w.shape[1]
    L = ML_CHUNK
    nc = S // L
    dh = C // ML_HEADS
    gb = jnp.zeros((1, LANES), F32).at[0, :ML_HEADS].set(b_i).at[0, ML_HEADS:2 * ML_HEADS].set(b_f)
    tri = (jnp.arange(L)[:, None] >= jnp.arange(L)[None, :]).astype(F32)
    full = lambda shp: pl.BlockSpec(shp, lambda b, c: (0,) * len(shp))
    blk = lambda j: pl.BlockSpec((L, C), lambda b, c: (b * nc + c, col0 + j))
    return pl.pallas_call(
        _mlstm_kernel,
        out_shape=jax.ShapeDtypeStruct((B * S, C), BF16),
        grid=(B, nc),
        in_specs=[blk(0), blk(1), blk(2),
                  pl.BlockSpec((L, LANES), lambda b, c: (b * nc + c, 0)),
                  full((CONV_WIDTH, C)), full((1, C)), full(wq.shape), full(wk.shape),
                  full((1, LANES)), full((1, C)), full((L, L))],
        out_specs=pl.BlockSpec((L, C), lambda b, c: (b * nc + c, 0)),
        scratch_shapes=[pltpu.VMEM((ML_HEADS, dh, dh), F32), pltpu.VMEM((ML_HEADS, 1, dh), F32),
                        pltpu.VMEM((ML_HEADS, 1, 1), F32), pltpu.VMEM((8, C), F32)],
        compiler_params=_cparams(("parallel", "arbitrary")),
        name="mlstm_mixer",
    )(proj, proj, proj, gates, conv_w, conv_b.reshape(1, C), wq.astype(BF16), wk.astype(BF16),
      gb, hnorm_g.reshape(1, C), tri)


def _sb_block(q_h, k2, v_h, tt, acc, valid):
    z = _dot_nt(q_h, k2)
    lb = _log_sigmoid(z)
    l1 = lb - z
    if valid is not None:
        l1 = jnp.where(valid, l1, 0.0)
    hi = l1.astype(BF16)
    lo = (l1 - hi.astype(F32)).astype(BF16)
    r = _dot(jnp.concatenate([hi, lo], axis=1), tt)
    kb = z.shape[1]
    a = jnp.exp(lb + r[:, :kb] + acc)
    if valid is not None:
        a = jnp.where(valid, a, 0.0)
    return _dot(a.astype(BF16), v_h), acc + r[:, kb:]


def _sb_kernel(q_ref, k_ref, v_ref, tt_ref, o_ref):
    qi = pl.program_id(2)
    blk = q_ref.shape[0]
    lane = lax.broadcasted_iota(jnp.int32, (1, LANES), 1)
    lo_half = lane < SB_HEAD_DIM
    q2 = q_ref[...]
    zero16 = jnp.zeros_like(q2)
    q_heads = (jnp.where(lo_half, q2, zero16), jnp.where(lo_half, zero16, q2))
    tt = tt_ref[...]

    def halves(v2):
        z16 = jnp.zeros_like(v2)
        return jnp.where(lo_half, v2, z16), jnp.where(lo_half, z16, v2)

    ri = lax.broadcasted_iota(jnp.int32, (blk, blk), 0)
    ci = lax.broadcasted_iota(jnp.int32, (blk, blk), 1)
    valid = ci < ri

    start = pl.multiple_of(qi * blk, blk)
    k2 = k_ref[pl.ds(start, blk), :]
    v_heads = halves(v_ref[pl.ds(start, blk), :])
    out = jnp.zeros((blk, LANES), F32)
    accs = []
    for hd in range(2):
        o_h, acc_h = _sb_block(q_heads[hd], k2, v_heads[hd], tt, jnp.zeros((blk, blk), F32), valid)
        out = out + o_h
        accs.append(acc_h)

    def body(step, carry):
        out, acc0, acc1 = carry
        st = pl.multiple_of((qi - 1 - step) * blk, blk)
        k2 = k_ref[pl.ds(st, blk), :]
        v_heads = halves(v_ref[pl.ds(st, blk), :])
        o0, acc0 = _sb_block(q_heads[0], k2, v_heads[0], tt, acc0, None)
        o1, acc1 = _sb_block(q_heads[1], k2, v_heads[1], tt, acc1, None)
        return out + o0 + o1, acc0, acc1

    out, _, _ = lax.fori_loop(0, qi, body, (out, accs[0], accs[1]))
    o_ref[...] = out.astype(o_ref.dtype)


def stickbreak_attention(qkv, *, B, S):
    W = SB_HEADS * SB_HEAD_DIM
    npair = W // LANES
    nb = S // SB_BLOCK
    blk = SB_BLOCK
    j = jnp.arange(blk)
    strict = (j[:, None] > j[None, :]).astype(BF16)
    half = jnp.concatenate([strict, jnp.ones((blk, blk), BF16)], axis=1)
    tt = jnp.concatenate([half, half], axis=0)
    return pl.pallas_call(
        _sb_kernel,
        out_shape=jax.ShapeDtypeStruct((B * S, W), BF16),
        grid=(B, npair, nb),
        in_specs=[pl.BlockSpec((blk, LANES), lambda b, p, i: (b * nb + i, p)),
                  pl.BlockSpec((S, LANES), lambda b, p, i: (b, npair + p)),
                  pl.BlockSpec((S, LANES), lambda b, p, i: (b, 2 * npair + p)),
                  pl.BlockSpec((2 * blk, 2 * blk), lambda b, p, i: (0, 0))],
        out_specs=pl.BlockSpec((blk, LANES), lambda b, p, i: (b * nb + i, p)),
        compiler_params=_cparams(("parallel", "parallel", "arbitrary")),
        name="stickbreak_attention",
    )(qkv, qkv, qkv, tt)


def _top_desc(x, n):
    out = []
    cur = x
    for it in range(n):
        mx = jnp.max(cur, axis=0, keepdims=True)
        out.append(mx)
        if it + 1 < n:
            cur = jnp.where(cur == mx, NEG_INF, cur)
    return out


_N_EXTRACT = PEER_TOPK + 1
_CAND = [(a, b) for a in range(_N_EXTRACT) for b in range(_N_EXTRACT) if (a + 1) * (b + 1) <= _N_EXTRACT]


def _peer_route_kernel(x_ref, g_ref, wq_ref, sk_ref, xn_ref, a1_ref, a2_ref, kt_ref):
    xn = _rms(x_ref[...], g_ref[...]).astype(BF16)
    xn_ref[...] = xn
    q = _dot(xn, wq_ref[...]).astype(BF16)
    sk1 = sk_ref[0]
    sk2 = sk_ref[1]
    for h in range(PEER_HEADS):
        q1 = q[:, (2 * h) * N_KEYS:(2 * h + 1) * N_KEYS]
        q2 = q[:, (2 * h + 1) * N_KEYS:(2 * h + 2) * N_KEYS]
        s1 = _dot_nt(sk1, q1)
        s2 = _dot_nt(sk2, q2)
        t1 = _top_desc(s1, _N_EXTRACT)
        t2 = _top_desc(s2, _N_EXTRACT)
        a1 = s1 - t1[0]
        a2 = s2 - t2[0]
        cand = jnp.concatenate([(t1[a] - t1[0]) + (t2[b] - t2[0]) for a, b in _CAND], axis=0)
        top = _top_desc(cand, _N_EXTRACT)
        z = jnp.exp(top[0])
        for kk in range(1, PEER_TOPK):
            z = z + jnp.exp(top[kk])
        log_z = jnp.log(z)
        a1_ref[h] = a1 - log_z
        a2_ref[h] = a2
        kt_ref[h] = jnp.exp(0.5 * (top[PEER_TOPK - 1] + top[PEER_TOPK]) - log_z)


def peer_route(h, g, wq, sub_keys, *, tm):
    T, D = h.shape
    H = PEER_HEADS
    return pl.pallas_call(
        _peer_route_kernel,
        out_shape=(jax.ShapeDtypeStruct((T, D), BF16),
                   jax.ShapeDtypeStruct((H, N_KEYS, T), F32),
                   jax.ShapeDtypeStruct((H, N_KEYS, T), F32),
                   jax.ShapeDtypeStruct((H, 1, T), F32)),
        grid=(T // tm,),
        in_specs=[pl.BlockSpec((tm, D), lambda i: (i, 0)),
                  pl.BlockSpec((1, D), lambda i: (0, 0)),
                  pl.BlockSpec(wq.shape, lambda i: (0, 0)),
                  pl.BlockSpec(sub_keys.shape, lambda i: (0, 0, 0))],
        out_specs=(pl.BlockSpec((tm, D), lambda i: (i, 0)),
                   pl.BlockSpec((H, N_KEYS, tm), lambda i: (0, 0, i)),
                   pl.BlockSpec((H, N_KEYS, tm), lambda i: (0, 0, i)),
                   pl.BlockSpec((H, 1, tm), lambda i: (0, 0, i))),
        compiler_params=_cparams(("parallel",)),
        name="peer_route",
    )(h, g.reshape(1, D), wq, sub_keys.astype(BF16))


def _peer_dense_kernel(xn_ref, a1_ref, a2_ref, kt_ref, u_ref, vt_ref, h_ref, o_ref, acc_ref, c_ref):
    e = pl.program_id(1)
    ec = u_ref.shape[0]
    tm = xn_ref.shape[0]
    n_i1 = ec // N_KEYS

    @pl.when(e == 0)
    def _():
        acc_ref[...] = jnp.zeros_like(acc_ref)

    st = _dot_nt(u_ref[...], xn_ref[...])
    for r in range(n_i1):
        for tc in range(tm // LANES):
            tl = slice(tc * LANES, (tc + 1) * LANES)
            w = jnp.zeros((N_KEYS, LANES), F32)
            for h in range(PEER_HEADS):
                gate = jnp.exp(a1_ref[h, r:r + 1, tl] + a2_ref[h, :, tl])
                w = w + jnp.where(gate >= kt_ref[h, :, tl], gate, 0.0)
            s = st[r * N_KEYS:(r + 1) * N_KEYS, tl]
            c_ref[r * N_KEYS:(r + 1) * N_KEYS, tl] = (_gelu(s) * w).astype(BF16)
    acc_ref[...] += _dot(vt_ref[...], c_ref[...])

    @pl.when(e == pl.num_programs(1) - 1)
    def _():
        o_ref[...] = h_ref[...] + acc_ref[...].T


def peer_dense(xn, a1, a2, kt, u16, vt16, h, *, tm, ec):
    T, D = h.shape
    NE = u16.shape[0]
    H = PEER_HEADS
    return pl.pallas_call(
        _peer_dense_kernel,
        out_shape=jax.ShapeDtypeStruct((T, D), F32),
        grid=(T // tm, NE // ec),
        in_specs=[pl.BlockSpec((tm, D), lambda i, e: (i, 0)),
                  pl.BlockSpec((H, ec // N_KEYS, tm), lambda i, e: (0, e, i)),
                  pl.BlockSpec((H, N_KEYS, tm), lambda i, e: (0, 0, i)),
                  pl.BlockSpec((H, 1, tm), lambda i, e: (0, 0, i)),
                  pl.BlockSpec((ec, D), lambda i, e: (e, 0)),
                  pl.BlockSpec((D, ec), lambda i, e: (0, e)),
                  pl.BlockSpec((tm, D), lambda i, e: (i, 0))],
        out_specs=pl.BlockSpec((tm, D), lambda i, e: (i, 0)),
        scratch_shapes=[pltpu.VMEM((D, tm), F32), pltpu.VMEM((ec, tm), BF16)],
        compiler_params=_cparams(("parallel", "arbitrary")),
        name="peer_dense",
    )(xn, a1, a2, kt, u16, vt16, h)


def _ple_kernel(h_ref, p_ref, wp_ref, g_ref, wg_ref, o_ref):
    h = h_ref[...]
    ple = _rms(_dot(p_ref[...].astype(BF16), wp_ref[...]), g_ref[...])
    gate = _sigmoid(_dot(h.astype(BF16), wg_ref[...]))
    o_ref[...] = h + gate * ple


def ple_embed(h, p, wp, g, wg, *, tm):
    T, D = h.shape
    P = p.shape[1]
    return pl.pallas_call(
        _ple_kernel,
        out_shape=jax.ShapeDtypeStruct((T, D), F32),
        grid=(T // tm,),
        in_specs=[pl.BlockSpec((tm, D), lambda i: (i, 0)),
                  pl.BlockSpec((tm, P), lambda i: (i, 0)),
                  pl.BlockSpec((P, D), lambda i: (0, 0)),
                  pl.BlockSpec((1, D), lambda i: (0, 0)),
                  pl.BlockSpec((D, D), lambda i: (0, 0))],
        out_specs=pl.BlockSpec((tm, D), lambda i: (i, 0)),
        compiler_params=_cparams(("parallel",)),
        name="ple_embed",
    )(h, p, wp, g.reshape(1, D), wg)


def even_layer_mixer(h, g, w_in, conv_a_w, conv_a_b, lru_wr, lru_br, lru_wi, lru_bi, lru_lam,
                     conv_m_w, conv_m_b, ml_wq, ml_wk, ml_bi, ml_bf, ml_hnorm_g, w_out, *, B, S,
                     tm, tn, ts):
    D = h.shape[1]
    C = conv_a_w.shape[1]
    n_main = w_in.shape[1] - 2 * ML_HEADS
    w_main = w_in[:, :n_main].astype(BF16)
    w_gate = jnp.pad(w_in[:, n_main:], ((0, 0), (0, LANES - 2 * ML_HEADS))).astype(BF16)
    proj = norm_matmul(h, g, w_main, tm=tm, tn=tn)
    gates = norm_matmul(h, g, w_gate, tm=tm, tn=LANES)
    out_a = rglru_mixer(proj, conv_a_w, conv_a_b, lru_wr, lru_br, lru_wi, lru_bi, lru_lam,
                        B=B, S=S, ts=ts)
    out_b = mlstm_mixer(proj, gates, conv_m_w, conv_m_b, ml_wq, ml_wk, ml_bi, ml_bf,
                        ml_hnorm_g.reshape(-1), B=B, S=S, col0=2)
    return matmul_residual([out_a, out_b], [w_out[:C].astype(BF16), w_out[C:].astype(BF16)], h, tm=tm)


def odd_layer_mixer(h, g, w_qkv, q_g, k_g, w_out, *, B, S, tm):
    qkv = qkv_proj(h, g, w_qkv.astype(BF16), q_g, k_g, tm=tm)
    o = stickbreak_attention(qkv, B=B, S=S)
    return matmul_residual([o], [w_out.astype(BF16)], h, tm=tm)


def peer_ffn(h, g, w_q, sub_keys, u, v, *, tm_route, tm, ec):
    xn, a1, a2, kt = peer_route(h, g, w_q.astype(BF16), sub_keys, tm=tm_route)
    return peer_dense(xn, a1, a2, kt, u.astype(BF16), v.T.astype(BF16), h, tm=tm, ec=ec)


def kernel(x, p, ln_mix_g, ln_ffn_g, w_in, conv_a_w, conv_a_b, lru_wr, lru_br, lru_wi, lru_bi, lru_lam, conv_m_w, conv_m_b, ml_wq, ml_wk, ml_bi, ml_bf, ml_hnorm_g, w_out_even, w_qkv, sb_q_g, sb_k_g, w_out_odd, peer_wq, peer_subkeys, peer_u, peer_v, ple_wp, ple_g, ple_wg):
    B, S, D = x.shape
    depth = p.shape[0]
    T = B * S
    h = x.reshape(T, D)
    for layer in range(depth):
        if layer % 2 == 0:
            e = layer // 2
            h = even_layer_mixer(h, ln_mix_g[layer], w_in[e], conv_a_w[e], conv_a_b[e], lru_wr[e],
                                 lru_br[e], lru_wi[e], lru_bi[e], lru_lam[e], conv_m_w[e],
                                 conv_m_b[e], ml_wq[e], ml_wk[e], ml_bi[e], ml_bf[e], ml_hnorm_g[e],
                                 w_out_even[e], B=B, S=S, tm=512, tn=1280, ts=256)
        else:
            o = layer // 2
            h = odd_layer_mixer(h, ln_mix_g[layer], w_qkv[o], sb_q_g[o], sb_k_g[o], w_out_odd[o],
                                B=B, S=S, tm=512)
        h = peer_ffn(h, ln_ffn_g[layer], peer_wq[layer], peer_subkeys[layer], peer_u[layer],
                     peer_v[layer], tm_route=256, tm=512, ec=1024)
        h = ple_embed(h, p[layer].reshape(T, -1), ple_wp[layer].astype(BF16), ple_g[layer],
                      ple_wg[layer].astype(BF16), tm=512)
    return h.reshape(B, S, D)
```

```python
import functools
import math

import jax
import jax.numpy as jnp
from jax import lax
from jax.experimental import pallas as pl
from jax.experimental.pallas import tpu as pltpu

F32 = jnp.float32
BF16 = jnp.bfloat16
EPS = 1e-6
NEG_INF = float("-inf")
LOG2E = 1.4426950408889634

LANES = 128
CONV_WIDTH = 4
LRU_BLOCKS = 8
LRU_C = 8.0
ML_HEADS = 4
ML_CHUNK = 128
SB_HEADS = 16
SB_HEAD_DIM = 64
SB_BLOCK = 128
SB_SUPER = 4
SB_QUERY_ROWS = 256
PEER_HEADS = 8
N_KEYS = 128
PEER_TOPK = 16
PEER_TILE_ROWS = 64
VMEM_LIMIT = 56 * 1024 * 1024


def _cparams(sem, flags=None):
    return pltpu.CompilerParams(dimension_semantics=sem, vmem_limit_bytes=VMEM_LIMIT, flags=flags)


def _rms(x, g):
    ms = jnp.mean(x * x, axis=-1, keepdims=True)
    return x * lax.rsqrt(ms + EPS) * g


def _gelu(x):
    c = math.sqrt(2.0 / math.pi)
    return 0.5 * x * (1.0 + jnp.tanh(c * (x + 0.044715 * (x * x * x))))


def _sigmoid(x):
    return 1.0 / (1.0 + jnp.exp(-x))


def _log_sigmoid(x):
    return jnp.minimum(x, 0.0) - jnp.log(1.0 + jnp.exp(-jnp.abs(x)))


def _dot(a, b):
    return jnp.dot(a, b, preferred_element_type=F32)


def _dot_nt(a, b):
    return lax.dot_general(a, b, (((1,), (1,)), ((), ())), preferred_element_type=F32)


def _dot_tn(a, b):
    return lax.dot_general(a, b, (((0,), (0,)), ((), ())), preferred_element_type=F32)


def _norm_mm_kernel(x_ref, g_ref, w_ref, o_ref, xn_ref):
    @pl.when(pl.program_id(1) == 0)
    def _():
        xn_ref[...] = _rms(x_ref[...], g_ref[...]).astype(BF16)

    o_ref[...] = _dot(xn_ref[...], w_ref[...]).astype(o_ref.dtype)


def norm_matmul(x, g, w, *, tm, tn, out_dtype=F32):
    T, D = x.shape
    N = w.shape[1]
    return pl.pallas_call(
        _norm_mm_kernel,
        out_shape=jax.ShapeDtypeStruct((T, N), out_dtype),
        grid=(T // tm, N // tn),
        in_specs=[pl.BlockSpec((tm, D), lambda i, j: (i, 0)),
                  pl.BlockSpec((1, D), lambda i, j: (0, 0)),
                  pl.BlockSpec((D, tn), lambda i, j: (0, j))],
        out_specs=pl.BlockSpec((tm, tn), lambda i, j: (i, j)),
        scratch_shapes=[pltpu.VMEM((tm, D), BF16)],
        compiler_params=_cparams(("parallel", "arbitrary")),
        name="norm_matmul",
    )(x, g.reshape(1, D), w)


def _qkv_kernel(x_ref, g_ref, w_ref, hg_ref, gm_ref, o_ref, xn_ref):
    j = pl.program_id(1)

    @pl.when(j == 0)
    def _():
        xn_ref[...] = _rms(x_ref[...], g_ref[...]).astype(BF16)

    y = _dot(xn_ref[...], w_ref[...])

    @pl.when(j < 2)
    def _():
        gm = gm_ref[...]
        hg = hg_ref[0]
        for c in range(y.shape[1] // LANES):
            yc = y[:, c * LANES:(c + 1) * LANES]
            ms = _dot((yc * yc).astype(BF16), gm)
            o_ref[:, c * LANES:(c + 1) * LANES] = (
                yc * lax.rsqrt(ms + EPS) * hg[:, c * LANES:(c + 1) * LANES]).astype(o_ref.dtype)

    @pl.when(j == 2)
    def _():
        o_ref[...] = y.astype(o_ref.dtype)


def qkv_proj(x, g, w, q_g, k_g, *, tm):
    T, D = x.shape
    W = SB_HEADS * SB_HEAD_DIM
    scale = SB_HEAD_DIM ** -0.5 * LOG2E
    hg = jnp.stack([jnp.tile(q_g, SB_HEADS) * scale, jnp.tile(k_g, SB_HEADS),
                    jnp.ones((W,), F32)]).reshape(3, 1, W)
    lane = jnp.arange(LANES)
    gm = ((lane[:, None] // SB_HEAD_DIM) == (lane[None, :] // SB_HEAD_DIM)).astype(BF16) / SB_HEAD_DIM
    return pl.pallas_call(
        _qkv_kernel,
        out_shape=jax.ShapeDtypeStruct((T, 3 * W), BF16),
        grid=(T // tm, 3),
        in_specs=[pl.BlockSpec((tm, D), lambda i, j: (i, 0)),
                  pl.BlockSpec((1, D), lambda i, j: (0, 0)),
                  pl.BlockSpec((D, W), lambda i, j: (0, j)),
                  pl.BlockSpec((1, 1, W), lambda i, j: (j, 0, 0)),
                  pl.BlockSpec((LANES, LANES), lambda i, j: (0, 0))],
        out_specs=pl.BlockSpec((tm, W), lambda i, j: (i, j)),
        scratch_shapes=[pltpu.VMEM((tm, D), BF16)],
        compiler_params=_cparams(("parallel", "arbitrary")),
        name="qkv_proj",
    )(x, g.reshape(1, D), w, hg, gm.astype(BF16))


def _mm_res_kernel(*refs, n_a):
    a_refs = refs[:n_a]
    w_refs = refs[n_a:2 * n_a]
    r_ref, o_ref = refs[2 * n_a], refs[2 * n_a + 1]
    acc = r_ref[...]
    for a_ref, w_ref in zip(a_refs, w_refs):
        acc = acc + _dot(a_ref[...].astype(BF16), w_ref[...])
    o_ref[...] = acc


def matmul_residual(a_list, w_list, res, *, tm):
    T, N = res.shape
    n_a = len(a_list)
    in_specs = ([pl.BlockSpec((tm, a.shape[1]), lambda i: (i, 0)) for a in a_list]
                + [pl.BlockSpec(w.shape, lambda i: (0, 0)) for w in w_list]
                + [pl.BlockSpec((tm, N), lambda i: (i, 0))])
    return pl.pallas_call(
        functools.partial(_mm_res_kernel, n_a=n_a),
        out_shape=jax.ShapeDtypeStruct((T, N), F32),
        grid=(T // tm,),
        in_specs=in_specs,
        out_specs=pl.BlockSpec((tm, N), lambda i: (i, 0)),
        compiler_params=_cparams(("parallel",)),
        name="matmul_residual",
    )(*a_list, *w_list, res)


def _conv4(x, tail, w, b):
    ts = x.shape[0]
    xe = jnp.concatenate([tail, x], axis=0)
    y = x * w[3:4, :] + b
    for k in range(1, CONV_WIDTH):
        y = y + pltpu.roll(xe, k, 0)[8:8 + ts, :] * w[3 - k:4 - k, :]
    return y


def _rglru_kernel(xa_ref, ya_ref, cw_ref, cb_ref, wr_ref, br_ref, wi_ref, bi_ref, lam_ref,
                  o_ref, h_ref, tail_ref, a_s, b_s):
    c = pl.program_id(1)
    ts, C = xa_ref.shape
    bs = C // LRU_BLOCKS

    @pl.when(c == 0)
    def _():
        h_ref[...] = jnp.zeros_like(h_ref)
        tail_ref[...] = jnp.zeros_like(tail_ref)

    xa = xa_ref[...]
    xc = _conv4(xa, tail_ref[...], cw_ref[...], cb_ref[...])
    tail_ref[...] = xa[ts - 8:, :]

    sp = jnp.log(1.0 + jnp.exp(-jnp.abs(lam_ref[...]))) + jnp.maximum(-lam_ref[...], 0.0)
    first = jnp.logical_and(c == 0, lax.broadcasted_iota(jnp.int32, (ts, 1), 0) == 0)
    for n in range(LRU_BLOCKS):
        sl = slice(n * bs, (n + 1) * bs)
        xb = xc[:, sl]
        xb16 = xb.astype(BF16)
        r = _sigmoid(_dot(xb16, wr_ref[n]) + br_ref[:, sl])
        ig = _sigmoid(_dot(xb16, wi_ref[n]) + bi_ref[:, sl])
        log_a = (-LRU_C) * r * sp[:, sl]
        a = jnp.exp(log_a)
        mult = jnp.sqrt(1.0 - jnp.exp(2.0 * log_a))
        mult = jnp.where(first, 1.0, mult)
        a_s[:, sl] = a
        b_s[:, sl] = mult * ig * xb

    def rows8(i, h):
        r0 = pl.multiple_of(i * 8, 8)
        a8 = a_s[pl.ds(r0, 8), :]
        b8 = b_s[pl.ds(r0, 8), :]
        hs = []
        for j in range(8):
            h = a8[j:j + 1, :] * h + b8[j:j + 1, :]
            hs.append(h)
        b_s[pl.ds(r0, 8), :] = jnp.concatenate(hs, axis=0)
        return h

    h_ref[...] = lax.fori_loop(0, ts // 8, rows8, h_ref[...])
    o_ref[...] = (b_s[...] * _gelu(ya_ref[...])).astype(o_ref.dtype)


def rglru_mixer(proj, conv_w, conv_b, wr, br, wi, bi, lam, *, B, S, ts):
    C = conv_w.shape[1]
    nc = S // ts
    row = lambda v: v.reshape(1, C)
    full = lambda shp: pl.BlockSpec(shp, lambda b, c: (0,) * len(shp))
    return pl.pallas_call(
        _rglru_kernel,
        out_shape=jax.ShapeDtypeStruct((B * S, C), BF16),
        grid=(B, nc),
        in_specs=[pl.BlockSpec((ts, C), lambda b, c: (b * nc + c, 0)),
                  pl.BlockSpec((ts, C), lambda b, c: (b * nc + c, 1)),
                  full((CONV_WIDTH, C)), full((1, C)),
                  full(wr.shape), full((1, C)), full(wi.shape), full((1, C)), full((1, C))],
        out_specs=pl.BlockSpec((ts, C), lambda b, c: (b * nc + c, 0)),
        scratch_shapes=[pltpu.VMEM((1, C), F32), pltpu.VMEM((8, C), F32),
                        pltpu.VMEM((ts, C), F32), pltpu.VMEM((ts, C), F32)],
        compiler_params=_cparams(("parallel", "arbitrary")),
        name="rglru_mixer",
    )(proj, proj, conv_w, row(conv_b), wr.astype(BF16), row(br), wi.astype(BF16), row(bi), row(lam))


def _mlstm_kernel(xm_ref, vm_ref, om_ref, gt_ref, cw_ref, cb_ref, wq_ref, wk_ref, gb_ref, hg_ref,
                  tri_ref, o_ref, ct_ref, n_ref, m_ref, tail_ref):
    c = pl.program_id(1)
    L, C = xm_ref.shape
    dh = C // ML_HEADS

    @pl.when(c == 0)
    def _():
        ct_ref[...] = jnp.zeros_like(ct_ref)
        n_ref[...] = jnp.zeros_like(n_ref)
        m_ref[...] = jnp.zeros_like(m_ref)
        tail_ref[...] = jnp.zeros_like(tail_ref)

    xm = xm_ref[...]
    pre = _conv4(xm, tail_ref[...], cw_ref[...], cb_ref[...])
    tail_ref[...] = xm[L - 8:, :]
    xc = pre * _sigmoid(pre)

    g = gt_ref[...] + gb_ref[...]
    lf = _log_sigmoid(g)
    bcum = jnp.dot(tri_ref[...], lf, preferred_element_type=F32,
                   precision=lax.Precision.HIGHEST)
    g_t = g.T
    b_t = bcum.T
    ri = lax.broadcasted_iota(jnp.int32, (L, L), 0)
    ci = lax.broadcasted_iota(jnp.int32, (L, L), 1)
    causal = ci <= ri

    for h in range(ML_HEADS):
        sl = slice(h * dh, (h + 1) * dh)
        fh = ML_HEADS + h
        b_col = bcum[:, fh:fh + 1]
        i_col = g[:, h:h + 1]
        b_row = b_t[fh:fh + 1, :]
        i_row = g_t[h:h + 1, :]
        g_last = bcum[L - 1:L, fh:fh + 1]
        m_prev = m_ref[h]

        x16 = xc[:, sl].astype(BF16)
        q = _dot(x16, wq_ref[h])
        k = _dot(x16, wk_ref[h]) * (dh ** -0.5)
        v = vm_ref[:, sl]
        q16 = q.astype(BF16)
        k16 = k.astype(BF16)

        dlog = jnp.where(causal, b_col - b_row + i_row, NEG_INF)
        inter = b_col + m_prev
        m_t = jnp.maximum(inter, jnp.max(dlog, axis=-1, keepdims=True))
        sc = _dot_nt(q16, k16) * jnp.exp(dlog - m_t)
        w_inter = jnp.exp(inter - m_t)
        ct = ct_ref[h]
        n_row = n_ref[h]
        num = _dot(sc.astype(BF16), v.astype(BF16)) + w_inter * _dot(q16, ct.astype(BF16))
        den = jnp.sum(sc, axis=-1, keepdims=True) + w_inter * jnp.sum(q * n_row, axis=-1, keepdims=True)
        hh = num / jnp.maximum(jnp.abs(den), jnp.exp(-m_t))

        decay = g_last - b_col + i_col
        m_new = jnp.maximum(g_last + m_prev, jnp.max(decay, axis=0, keepdims=True))
        ws = jnp.exp(decay - m_new)
        wc = jnp.exp(g_last + m_prev - m_new)
        ct_ref[h] = wc * ct + _dot_tn(k16, (ws * v).astype(BF16))
        n_ref[h] = wc * n_row + jnp.sum(ws * k, axis=0, keepdims=True)
        m_ref[h] = m_new

        hn = _rms(hh, hg_ref[:, sl])
        o_ref[:, sl] = (_sigmoid(om_ref[:, sl]) * hn).astype(o_ref.dtype)


def mlstm_mixer(proj, gates, conv_w, conv_b, wq, wk, b_i, b_f, hnorm_g, *, B, S, col0):
    C = conv_w.shape[1]
    L = ML_CHUNK
    nc = S // L
    dh = C // ML_HEADS
    gb = jnp.zeros((1, LANES), F32).at[0, :ML_HEADS].set(b_i).at[0, ML_HEADS:2 * ML_HEADS].set(b_f)
    tri = (jnp.arange(L)[:, None] >= jnp.arange(L)[None, :]).astype(F32)
    full = lambda shp: pl.BlockSpec(shp, lambda b, c: (0,) * len(shp))
    blk = lambda j: pl.BlockSpec((L, C), lambda b, c: (b * nc + c, col0 + j))
    return pl.pallas_call(
        _mlstm_kernel,
        out_shape=jax.ShapeDtypeStruct((B * S, C), BF16),
        grid=(B, nc),
        in_specs=[blk(0), blk(1), blk(2),
                  pl.BlockSpec((L, LANES), lambda b, c: (b * nc + c, 0)),
                  full((CONV_WIDTH, C)), full((1, C)), full(wq.shape), full(wk.shape),
                  full((1, LANES)), full((1, C)), full((L, L))],
        out_specs=pl.BlockSpec((L, C), lambda b, c: (b * nc + c, 0)),
        scratch_shapes=[pltpu.VMEM((ML_HEADS, dh, dh), F32), pltpu.VMEM((ML_HEADS, 1, dh), F32),
                        pltpu.VMEM((ML_HEADS, 1, 1), F32), pltpu.VMEM((8, C), F32)],
        compiler_params=_cparams(("parallel", "arbitrary")),
        name="mlstm_mixer",
    )(proj, proj, proj, gates, conv_w, conv_b.reshape(1, C), wq.astype(BF16), wk.astype(BF16),
      gb, hnorm_g.reshape(1, C), tri)


def _sb_super(q_h, k_sb, v_h, tt, acc, valid):
    blk = tt.shape[0] // 2
    n_sub = k_sb.shape[0] // blk
    z = _dot_nt(q_h, k_sb)
    lb = jnp.minimum(z, 0.0) - jnp.log2(1.0 + jnp.exp2(-jnp.abs(z)))
    l1 = lb - z
    if valid is not None:
        l1 = jnp.where(valid, l1, 0.0)
    hi = l1.astype(BF16)
    lo = (l1 - hi.astype(F32)).astype(BF16)
    parts = [None] * n_sub
    for j in reversed(range(n_sub)):
        sl = slice(j * blk, (j + 1) * blk)
        r = _dot(jnp.concatenate([hi[:, sl], lo[:, sl]], axis=1), tt)
        a = jnp.exp2(lb[:, sl] + r[:, :blk] + acc)
        if valid is not None:
            a = jnp.where(valid[:, sl], a, 0.0)
        parts[j] = a.astype(BF16)
        acc = acc + r[:, blk:]
    return _dot(jnp.concatenate(parts, axis=1), v_h), acc


def _sb_kernel(q_ref, k_ref, v_ref, tt_ref, o_ref, *, n_sub):
    qi = pl.program_id(2)
    qb = q_ref.shape[0]
    blk = tt_ref.shape[0] // 2
    sup = n_sub * blk
    ratio = sup // qb
    lane = lax.broadcasted_iota(jnp.int32, (1, LANES), 1)
    lo_half = lane < SB_HEAD_DIM
    q2 = q_ref[...]
    zero16 = jnp.zeros_like(q2)
    q_heads = (jnp.where(lo_half, q2, zero16), jnp.where(lo_half, zero16, q2))
    tt = tt_ref[...]

    def load(sb):
        st = pl.multiple_of(sb * sup, sup)
        v2 = v_ref[pl.ds(st, sup), :]
        z16 = jnp.zeros_like(v2)
        return k_ref[pl.ds(st, sup), :], (jnp.where(lo_half, v2, z16), jnp.where(lo_half, z16, v2))

    sd = qi // ratio
    ri = lax.broadcasted_iota(jnp.int32, (qb, sup), 0)
    ci = lax.broadcasted_iota(jnp.int32, (qb, sup), 1)
    valid = ci < ri + (qi - sd * ratio) * qb
    k_sb, v_heads = load(sd)
    zero_acc = jnp.zeros((qb, blk), F32)
    o0, acc0 = _sb_super(q_heads[0], k_sb, v_heads[0], tt, zero_acc, valid)
    o1, acc1 = _sb_super(q_heads[1], k_sb, v_heads[1], tt, zero_acc, valid)

    def body(step, carry):
        out, acc0, acc1 = carry
        k_sb, v_heads = load(sd - 1 - step)
        o0, acc0 = _sb_super(q_heads[0], k_sb, v_heads[0], tt, acc0, None)
        o1, acc1 = _sb_super(q_heads[1], k_sb, v_heads[1], tt, acc1, None)
        return out + o0 + o1, acc0, acc1

    out, _, _ = lax.fori_loop(0, sd, body, (o0 + o1, acc0, acc1))
    o_ref[...] = out.astype(o_ref.dtype)


def stickbreak_attention(qkv, *, B, S):
    W = SB_HEADS * SB_HEAD_DIM
    npair = W // LANES
    nb = S // SB_BLOCK
    blk = SB_BLOCK
    j = jnp.arange(blk)
    strict = (j[:, None] > j[None, :]).astype(BF16)
    half = jnp.concatenate([strict, jnp.ones((blk, blk), BF16)], axis=1)
    tt = jnp.concatenate([half, half], axis=0)
    n_sub = math.gcd(nb, SB_SUPER)
    qb = math.gcd(n_sub * blk, SB_QUERY_ROWS)
    nq = S // qb
    return pl.pallas_call(
        functools.partial(_sb_kernel, n_sub=n_sub),
        out_shape=jax.ShapeDtypeStruct((B * S, W), BF16),
        grid=(B, npair, nq),
        in_specs=[pl.BlockSpec((qb, LANES), lambda b, p, i: (b * nq + i, p)),
                  pl.BlockSpec((S, LANES), lambda b, p, i: (b, npair + p)),
                  pl.BlockSpec((S, LANES), lambda b, p, i: (b, 2 * npair + p)),
                  pl.BlockSpec((2 * blk, 2 * blk), lambda b, p, i: (0, 0))],
        out_specs=pl.BlockSpec((qb, LANES), lambda b, p, i: (b * nq + i, p)),
        compiler_params=_cparams(("parallel", "parallel", "arbitrary")),
        name="stickbreak_attention",
    )(qkv, qkv, qkv, tt)


def _top_desc(x, n):
    out = []
    cur = x
    for it in range(n):
        mx = jnp.max(cur, axis=0, keepdims=True)
        out.append(mx)
        if it + 1 < n:
            cur = jnp.where(cur == mx, NEG_INF, cur)
    return out


_N_EXTRACT = PEER_TOPK + 1
_CAND = [(a, b) for a in range(_N_EXTRACT) for b in range(_N_EXTRACT) if (a + 1) * (b + 1) <= _N_EXTRACT]


def _peer_route_kernel(x_ref, g_ref, wq_ref, sk_ref, xn_ref, a1_ref, a2_ref, kt_ref):
    xn = _rms(x_ref[...], g_ref[...]).astype(BF16)
    xn_ref[...] = xn
    q = _dot(xn, wq_ref[...]).astype(BF16)
    sk1 = sk_ref[0]
    sk2 = sk_ref[1]
    for h in range(PEER_HEADS):
        q1 = q[:, (2 * h) * N_KEYS:(2 * h + 1) * N_KEYS]
        q2 = q[:, (2 * h + 1) * N_KEYS:(2 * h + 2) * N_KEYS]
        s1 = _dot_nt(sk1, q1)
        s2 = _dot_nt(sk2, q2)
        t1 = _top_desc(s1, _N_EXTRACT)
        t2 = _top_desc(s2, _N_EXTRACT)
        a1 = s1 - t1[0]
        a2 = s2 - t2[0]
        cand = jnp.concatenate([(t1[a] - t1[0]) + (t2[b] - t2[0]) for a, b in _CAND], axis=0)
        top = _top_desc(cand, _N_EXTRACT)
        z = jnp.exp(top[0])
        for kk in range(1, PEER_TOPK):
            z = z + jnp.exp(top[kk])
        log_z = jnp.log(z)
        a1_ref[h] = jnp.exp(a1 - log_z)
        a2_ref[h] = jnp.exp(a2)
        kt_ref[h] = jnp.exp(0.5 * (top[PEER_TOPK - 1] + top[PEER_TOPK]) - log_z)


def peer_route(h, g, wq, sub_keys, *, tm):
    T, D = h.shape
    H = PEER_HEADS
    return pl.pallas_call(
        _peer_route_kernel,
        out_shape=(jax.ShapeDtypeStruct((T, D), BF16),
                   jax.ShapeDtypeStruct((H, N_KEYS, T), F32),
                   jax.ShapeDtypeStruct((H, N_KEYS, T), F32),
                   jax.ShapeDtypeStruct((H, 1, T), F32)),
        grid=(T // tm,),
        in_specs=[pl.BlockSpec((tm, D), lambda i: (i, 0)),
                  pl.BlockSpec((1, D), lambda i: (0, 0)),
                  pl.BlockSpec(wq.shape, lambda i: (0, 0)),
                  pl.BlockSpec(sub_keys.shape, lambda i: (0, 0, 0))],
        out_specs=(pl.BlockSpec((tm, D), lambda i: (i, 0)),
                   pl.BlockSpec((H, N_KEYS, tm), lambda i: (0, 0, i)),
                   pl.BlockSpec((H, N_KEYS, tm), lambda i: (0, 0, i)),
                   pl.BlockSpec((H, 1, tm), lambda i: (0, 0, i))),
        compiler_params=_cparams(("parallel",)),
        name="peer_route",
    )(h, g.reshape(1, D), wq, sub_keys.astype(BF16))


_GELU_C = math.sqrt(2.0 / math.pi)


def _peer_coef(s, w):
    inner = s * (_GELU_C + (_GELU_C * 0.044715) * (s * s))
    hw = s * (0.5 * w)
    return hw + hw * jnp.tanh(inner)


def _peer_dense_kernel(xn_ref, e1_ref, e2_ref, kt_ref, u_ref, vt_ref, h_ref, o_ref, acc_ref, *, tsub):
    e = pl.program_id(1)
    ec = u_ref.shape[0]
    tm = xn_ref.shape[0]
    n_i1 = ec // N_KEYS

    @pl.when(e == 0)
    def _():
        acc_ref[...] = jnp.zeros_like(acc_ref)

    for k in range(tm // tsub):
        st = _dot_nt(u_ref[...], xn_ref[k * tsub:(k + 1) * tsub, :])
        rows_c = []
        for rr in range(ec // PEER_TILE_ROWS):
            r = rr * PEER_TILE_ROWS // N_KEYS
            k2 = slice(rr * PEER_TILE_ROWS % N_KEYS, rr * PEER_TILE_ROWS % N_KEYS + PEER_TILE_ROWS)
            cols_c = []
            for tc in range(tsub // LANES):
                tl = slice(k * tsub + tc * LANES, k * tsub + (tc + 1) * LANES)
                w = jnp.zeros((PEER_TILE_ROWS, LANES), F32)
                for h in range(PEER_HEADS):
                    gate = e1_ref[h, r:r + 1, tl] * e2_ref[h, k2, tl]
                    w = w + jnp.where(gate >= kt_ref[h, :, tl], gate, 0.0)
                s = st[rr * PEER_TILE_ROWS:(rr + 1) * PEER_TILE_ROWS, tc * LANES:(tc + 1) * LANES]
                cols_c.append(_peer_coef(s, w).astype(BF16))
            rows_c.append(jnp.concatenate(cols_c, axis=1))
        c = jnp.concatenate(rows_c, axis=0)
        acc_ref[:, k * tsub:(k + 1) * tsub] += _dot(vt_ref[...], c)

    @pl.when(e == pl.num_programs(1) - 1)
    def _():
        o_ref[...] = h_ref[...] + acc_ref[...].T


def peer_dense(xn, e1, e2, kt, u16, vt16, h, *, tm, ec, tsub):
    T, D = h.shape
    NE = u16.shape[0]
    H = PEER_HEADS
    return pl.pallas_call(
        functools.partial(_peer_dense_kernel, tsub=tsub),
        out_shape=jax.ShapeDtypeStruct((T, D), F32),
        grid=(T // tm, NE // ec),
        in_specs=[pl.BlockSpec((tm, D), lambda i, e: (i, 0)),
                  pl.BlockSpec((H, ec // N_KEYS, tm), lambda i, e: (0, e, i)),
                  pl.BlockSpec((H, N_KEYS, tm), lambda i, e: (0, 0, i)),
                  pl.BlockSpec((H, 1, tm), lambda i, e: (0, 0, i)),
                  pl.BlockSpec((ec, D), lambda i, e: (e, 0)),
                  pl.BlockSpec((D, ec), lambda i, e: (0, e)),
                  pl.BlockSpec((tm, D), lambda i, e: (i, 0))],
        out_specs=pl.BlockSpec((tm, D), lambda i, e: (i, 0)),
        scratch_shapes=[pltpu.VMEM((D, tm), F32)],
        compiler_params=_cparams(("parallel", "arbitrary")),
        name="peer_dense",
    )(xn, e1, e2, kt, u16, vt16, h)


def _ple_kernel(h_ref, p_ref, wp_ref, g_ref, wg_ref, o_ref):
    h = h_ref[...]
    ple = _rms(_dot(p_ref[...].astype(BF16), wp_ref[...]), g_ref[...])
    gate = _sigmoid(_dot(h.astype(BF16), wg_ref[...]))
    o_ref[...] = h + gate * ple


def ple_embed(h, p, wp, g, wg, *, tm):
    T, D = h.shape
    P = p.shape[1]
    return pl.pallas_call(
        _ple_kernel,
        out_shape=jax.ShapeDtypeStruct((T, D), F32),
        grid=(T // tm,),
        in_specs=[pl.BlockSpec((tm, D), lambda i: (i, 0)),
                  pl.BlockSpec((tm, P), lambda i: (i, 0)),
                  pl.BlockSpec((P, D), lambda i: (0, 0)),
                  pl.BlockSpec((1, D), lambda i: (0, 0)),
                  pl.BlockSpec((D, D), lambda i: (0, 0))],
        out_specs=pl.BlockSpec((tm, D), lambda i: (i, 0)),
        compiler_params=_cparams(("parallel",)),
        name="ple_embed",
    )(h, p, wp, g.reshape(1, D), wg)


def even_layer_mixer(h, g, w_in, conv_a_w, conv_a_b, lru_wr, lru_br, lru_wi, lru_bi, lru_lam,
                     conv_m_w, conv_m_b, ml_wq, ml_wk, ml_bi, ml_bf, ml_hnorm_g, w_out, *, B, S,
                     tm, tn, ts):
    D = h.shape[1]
    C = conv_a_w.shape[1]
    n_main = w_in.shape[1] - 2 * ML_HEADS
    w_main = w_in[:, :n_main].astype(BF16)
    w_gate = jnp.pad(w_in[:, n_main:], ((0, 0), (0, LANES - 2 * ML_HEADS))).astype(BF16)
    proj = norm_matmul(h, g, w_main, tm=tm, tn=tn)
    gates = norm_matmul(h, g, w_gate, tm=tm, tn=LANES)
    out_a = rglru_mixer(proj, conv_a_w, conv_a_b, lru_wr, lru_br, lru_wi, lru_bi, lru_lam,
                        B=B, S=S, ts=ts)
    out_b = mlstm_mixer(proj, gates, conv_m_w, conv_m_b, ml_wq, ml_wk, ml_bi, ml_bf,
                        ml_hnorm_g.reshape(-1), B=B, S=S, col0=2)
    return matmul_residual([out_a, out_b], [w_out[:C].astype(BF16), w_out[C:].astype(BF16)], h, tm=tm)


def odd_layer_mixer(h, g, w_qkv, q_g, k_g, w_out, *, B, S, tm):
    qkv = qkv_proj(h, g, w_qkv.astype(BF16), q_g, k_g, tm=tm)
    o = stickbreak_attention(qkv, B=B, S=S)
    return matmul_residual([o], [w_out.astype(BF16)], h, tm=tm)


def peer_ffn(h, g, w_q, sub_keys, u, v, *, tm_route, tm, ec, tsub):
    xn, e1, e2, kt = peer_route(h, g, w_q.astype(BF16), sub_keys, tm=tm_route)
    return peer_dense(xn, e1, e2, kt, u.astype(BF16), v.T.astype(BF16), h, tm=tm, ec=ec, tsub=tsub)


def kernel(x, p, ln_mix_g, ln_ffn_g, w_in, conv_a_w, conv_a_b, lru_wr, lru_br, lru_wi, lru_bi, lru_lam, conv_m_w, conv_m_b, ml_wq, ml_wk, ml_bi, ml_bf, ml_hnorm_g, w_out_even, w_qkv, sb_q_g, sb_k_g, w_out_odd, peer_wq, peer_subkeys, peer_u, peer_v, ple_wp, ple_g, ple_wg):
    B, S, D = x.shape
    depth = p.shape[0]
    T = B * S
    h = x.reshape(T, D)
    for layer in range(depth):
        if layer % 2 == 0:
            e = layer // 2
            h = even_layer_mixer(h, ln_mix_g[layer], w_in[e], conv_a_w[e], conv_a_b[e], lru_wr[e],
                                 lru_br[e], lru_wi[e], lru_bi[e], lru_lam[e], conv_m_w[e],
                                 conv_m_b[e], ml_wq[e], ml_wk[e], ml_bi[e], ml_bf[e], ml_hnorm_g[e],
                                 w_out_even[e], B=B, S=S, tm=512, tn=1280, ts=256)
        else:
            o = layer // 2
            h = odd_layer_mixer(h, ln_mix_g[layer], w_qkv[o], sb_q_g[o], sb_k_g[o], w_out_odd[o],
                                B=B, S=S, tm=512)
        h = peer_ffn(h, ln_ffn_g[layer], peer_wq[layer], peer_subkeys[layer], peer_u[layer],
                     peer_v[layer], tm_route=256, tm=1024, ec=1024, tsub=256)
        h = ple_embed(h, p[layer].reshape(T, -1), ple_wp[layer].astype(BF16), ple_g[layer],
                      ple_wg[layer].astype(BF16), tm=512)
    return h.reshape(B, S, D)
```
